```python
import jax, jax.numpy as jnp
from jax import lax
import numpy as np

D_MODEL = 1024
BATCH = 8
SEQ = 4096
DEPTH = 1

N_MEM = 256
HG_HEADS = 4
HG_DK = 128
HG_DV = 128
ML_HEADS = 4
ML_DK = 128
ML_DV = 128
D_HG = HG_HEADS * HG_DV
D_ML = ML_HEADS * ML_DV
D_MIX = D_HG + D_ML
CHUNK = 64
ML_CONV = 4
FFN_CONV = 3
D_FF = 2816
CA_HEADS = 4
CA_DH = D_MODEL // CA_HEADS
ALPHA = (2.0 * DEPTH) ** 0.25
BETA = (8.0 * DEPTH) ** -0.25
LN_EPS = 1e-5
NEG_BIG = -1e30

IN_SIZES = (HG_HEADS * HG_DK, HG_HEADS * HG_DK, D_HG, D_HG,
            ML_HEADS * ML_DK, ML_HEADS * ML_DK, D_ML, D_ML, ML_HEADS, ML_HEADS)
IN_SPLITS = tuple(int(c) for c in np.cumsum(IN_SIZES)[:-1])
D_IN = int(sum(IN_SIZES))
FG_START = int(sum(IN_SIZES[:-1]))

kernel_name = 'hybrid_hgrn2_mlstm_deepnorm'


def layer_norm(x, g, b):
    xf = x.astype(jnp.float32)
    mu = jnp.mean(xf, axis=-1, keepdims=True)
    var = jnp.mean(jnp.square(xf - mu), axis=-1, keepdims=True)
    return ((xf - mu) * lax.rsqrt(var + LN_EPS) * g + b).astype(x.dtype)


def head_rms_norm(h, w):
    y = h * lax.rsqrt(jnp.mean(h * h, axis=-1, keepdims=True) + LN_EPS)
    return y.reshape(*h.shape[:-2], -1) * w


def head_layer_norm(h, w):
    mu = jnp.mean(h, axis=-1, keepdims=True)
    var = jnp.mean(jnp.square(h - mu), axis=-1, keepdims=True)
    y = (h - mu) * lax.rsqrt(var + LN_EPS)
    return y.reshape(*h.shape[:-2], -1) * w


def causal_dwconv(x, w, b):
    k_w = w.shape[0]
    s = x.shape[1]
    xp = jnp.pad(x, ((0, 0), (k_w - 1, 0), (0, 0)))
    y = b
    for j in range(k_w):
        y = y + xp[:, j:j + s] * w[j]
    return y


def to_chunks(t):
    bsz, s, h = t.shape[:3]
    t = t.reshape(bsz, s // CHUNK, CHUNK, h, *t.shape[3:])
    return jnp.moveaxis(jnp.moveaxis(t, 1, 0), 3, 2)


def from_chunks(t):
    t = jnp.moveaxis(jnp.moveaxis(t, 2, 3), 0, 1)
    return t.reshape(t.shape[0], t.shape[1] * t.shape[2], *t.shape[3:])


def hgrn2_chunkwise(q, k, v, log_f):
    bsz, _, h, dk = q.shape
    dv = v.shape[-1]
    mask = jnp.tril(jnp.ones((CHUNK, CHUNK), dtype=bool))

    def step(state, inp):
        q_, k_, v_, lf = inp
        b = jnp.cumsum(lf, axis=2)
        b_ref = b[:, :, CHUNK // 2 - 1:CHUNK // 2]
        attn = jnp.einsum('bhtd,bhsd->bhts', q_ * jnp.exp(b - b_ref), k_ * jnp.exp(b_ref - b))
        attn = jnp.where(mask, attn, 0.0)
        o = (jnp.einsum('bhts,bhsv->bhtv', attn, v_)
             + jnp.einsum('bhtd,bhdv->bhtv', q_ * jnp.exp(b), state))
        b_last = b[:, :, -1:]
        state = (jnp.exp(b_last)[:, :, 0, :, None] * state
                 + jnp.einsum('bhsd,bhsv->bhdv', k_ * jnp.exp(b_last - b), v_))
        return state, o

    s0 = jnp.zeros((bsz, h, dk, dv), jnp.float32)
    _, o = lax.scan(step, s0, (to_chunks(q), to_chunks(k), to_chunks(v), to_chunks(log_f)))
    return from_chunks(o)


def mlstm_chunkwise(q, k, v, i_log, f_log):
    bsz, _, h, dk = q.shape
    dv = v.shape[-1]
    mask = jnp.tril(jnp.ones((CHUNK, CHUNK), dtype=bool))

    def step(carry, inp):
        c_st, n_st, m_st = carry
        q_, k_, v_, ig, lf = inp
        b = jnp.cumsum(lf, axis=-1)
        g = b[..., -1]
        d = jnp.where(mask, b[..., :, None] - b[..., None, :] + ig[..., None, :], -jnp.inf)
        inter = b + m_st[..., None]
        m_t = jnp.maximum(inter, jnp.max(d, axis=-1))
        w = jnp.exp(d - m_t[..., None])
        s = jnp.einsum('bhtd,bhsd->bhts', q_, k_) * w
        w_inter = jnp.exp(inter - m_t)
        num = (jnp.einsum('bhts,bhsv->bhtv', s, v_)
               + w_inter[..., None] * jnp.einsum('bhtd,bhdv->bhtv', q_, c_st))
        den = jnp.sum(s, axis=-1) + w_inter * jnp.einsum('bhtd,bhd->bht', q_, n_st)
        h_out = num / jnp.maximum(jnp.abs(den), jnp.exp(-m_t))[..., None]
        a = g[..., None] - b + ig
        m_new = jnp.maximum(g + m_st, jnp.max(a, axis=-1))
        decay = jnp.exp(g + m_st - m_new)
        wk = k_ * jnp.exp(a - m_new[..., None])[..., None]
        c_st = decay[..., None, None] * c_st + jnp.einsum('bhsd,bhsv->bhdv', wk, v_)
        n_st = decay[..., None] * n_st + jnp.sum(wk, axis=2)
        return (c_st, n_st, m_new), h_out

    init = (jnp.zeros((bsz, h, dk, dv), jnp.float32),
            jnp.zeros((bsz, h, dk), jnp.float32),
            jnp.full((bsz, h), NEG_BIG, jnp.float32))
    _, o = lax.scan(step, init, (to_chunks(q), to_chunks(k), to_chunks(v),
                                 to_chunks(i_log), to_chunks(f_log)))
    return from_chunks(o)


def hybrid_mixer(x, w_in, b_in, lb, hg_norm_w, ml_conv_w, ml_conv_b, ml_norm_w, w_out):
    bsz, s, _ = x.shape
    proj = x @ w_in + b_in
    hq, hf, hi, hg, mq, mk, mv, mo, mi, mf = jnp.split(proj, IN_SPLITS, axis=-1)
    f32 = lambda t: t.astype(jnp.float32)
    heads = lambda t, nh: t.reshape(bsz, s, nh, -1)
    sig = jax.nn.sigmoid(f32(hf))
    log_f = jnp.log(lb + (1.0 - lb) * sig)
    k_in = (1.0 - lb) * jax.nn.sigmoid(-f32(hf))
    o_hg = hgrn2_chunkwise(heads(jax.nn.silu(f32(hq)), HG_HEADS), heads(k_in, HG_HEADS),
                           heads(f32(hi), HG_HEADS), heads(log_f, HG_HEADS))
    o_hg = head_rms_norm(o_hg, hg_norm_w) * jax.nn.silu(f32(hg))
    qk = jax.nn.silu(f32(causal_dwconv(jnp.concatenate([mq, mk], axis=-1), ml_conv_w, ml_conv_b)))
    q_ml, k_ml = jnp.split(qk, 2, axis=-1)
    h_ml = mlstm_chunkwise(heads(q_ml, ML_HEADS) * (ML_DK ** -0.5), heads(k_ml, ML_HEADS),
                           heads(f32(mv), ML_HEADS), f32(mi), jax.nn.log_sigmoid(f32(mf)))
    o_ml = jax.nn.sigmoid(f32(mo)) * head_layer_norm(h_ml, ml_norm_w)
    y = jnp.concatenate([o_hg, o_ml], axis=-1).astype(x.dtype)
    return y @ w_out


def memory_cross_attention(x, mem, wq, wkv, wo):
    bsz, s, d = x.shape
    q = (x @ wq).reshape(bsz, s, CA_HEADS, CA_DH)
    k, v = jnp.split(mem @ wkv, 2, axis=-1)
    k = k.reshape(bsz, -1, CA_HEADS, CA_DH)
    v = v.reshape(bsz, -1, CA_HEADS, CA_DH)
    sc = jnp.einsum('bshd,bmhd->bhsm', q, k).astype(jnp.float32) * (CA_DH ** -0.5)
    p = jax.nn.softmax(sc, axis=-1).astype(v.dtype)
    o = jnp.einsum('bhsm,bmhd->bshd', p, v).reshape(bsz, s, d)
    return o @ wo


def conv_ffn(x, w_up, conv_w, conv_b, w_down):
    u = causal_dwconv(x @ w_up, conv_w, conv_b)
    gate, val = jnp.split(u, 2, axis=-1)
    return (jax.nn.gelu(gate) * val) @ w_down


def setup_inputs(seed: int = 0) -> dict:
    key = jax.random.key(seed)
    ks = jax.random.split(key, 24)
    nrm = lambda k, shape, scale: jax.random.normal(k, shape, jnp.float32) * scale
    b_in = nrm(ks[3], (DEPTH, D_IN), 0.02)
    b_in = b_in.at[:, FG_START:].add(jnp.linspace(3.0, 6.0, ML_HEADS, dtype=jnp.float32))
    return {
        'x': nrm(ks[0], (BATCH, SEQ, D_MODEL), 1.0),
        'mem': nrm(ks[1], (BATCH, N_MEM, D_MODEL), 1.0),
        'w_in': nrm(ks[2], (DEPTH, D_MODEL, D_IN), D_MODEL ** -0.5),
        'b_in': b_in,
        'hg_lb_logits': 1.0 + nrm(ks[4], (DEPTH + 1, D_HG), 0.3),
        'hg_norm_w': 1.0 + nrm(ks[5], (DEPTH, D_HG), 0.02),
        'ml_conv_w': nrm(ks[6], (DEPTH, ML_CONV, 2 * D_ML), ML_CONV ** -0.5),
        'ml_conv_b': nrm(ks[7], (DEPTH, 2 * D_ML), 0.02),
        'ml_norm_w': 1.0 + nrm(ks[8], (DEPTH, D_ML), 0.02),
        'w_out': nrm(ks[9], (DEPTH, D_MIX, D_MODEL), BETA * D_MIX ** -0.5),
        'ln1_g': 1.0 + nrm(ks[10], (DEPTH, D_MODEL), 0.02),
        'ln1_b': nrm(ks[11], (DEPTH, D_MODEL), 0.02),
        'ca_wq': nrm(ks[12], (DEPTH, D_MODEL, D_MODEL), D_MODEL ** -0.5),
        'ca_wkv': nrm(ks[13], (DEPTH, D_MODEL, 2 * D_MODEL), D_MODEL ** -0.5),
        'ca_wo': nrm(ks[14], (DEPTH, D_MODEL, D_MODEL), BETA * D_MODEL ** -0.5),
        'ln2_g': 1.0 + nrm(ks[15], (DEPTH, D_MODEL), 0.02),
        'ln2_b': nrm(ks[16], (DEPTH, D_MODEL), 0.02),
        'ffn_w_up': nrm(ks[17], (DEPTH, D_MODEL, 2 * D_FF), D_MODEL ** -0.5),
        'ffn_conv_w': nrm(ks[18], (DEPTH, FFN_CONV, 2 * D_FF), FFN_CONV ** -0.5),
        'ffn_conv_b': nrm(ks[19], (DEPTH, 2 * D_FF), 0.02),
        'ffn_w_down': nrm(ks[20], (DEPTH, D_FF, D_MODEL), BETA * D_FF ** -0.5),
        'ln3_g': 1.0 + nrm(ks[21], (DEPTH, D_MODEL), 0.02),
        'ln3_b': nrm(ks[22], (DEPTH, D_MODEL), 0.02),
    }


def reference(x, mem, w_in, b_in, hg_lb_logits, hg_norm_w, ml_conv_w, ml_conv_b, ml_norm_w,
              w_out, ln1_g, ln1_b, ca_wq, ca_wkv, ca_wo, ln2_g, ln2_b,
              ffn_w_up, ffn_conv_w, ffn_conv_b, ffn_w_down, ln3_g, ln3_b):
    lower_bounds = jnp.cumsum(jax.nn.softmax(hg_lb_logits.astype(jnp.float32), axis=0), axis=0)
    for l in range(DEPTH):
        mix = hybrid_mixer(x, w_in[l], b_in[l], lower_bounds[l], hg_norm_w[l],
                           ml_conv_w[l], ml_conv_b[l], ml_norm_w[l], w_out[l])
        x = layer_norm(ALPHA * x + mix, ln1_g[l], ln1_b[l])
        ca = memory_cross_attention(x, mem, ca_wq[l], ca_wkv[l], ca_wo[l])
        x = layer_norm(ALPHA * x + ca, ln2_g[l], ln2_b[l])
        ff = conv_ffn(x, ffn_w_up[l], ffn_conv_w[l], ffn_conv_b[l], ffn_w_down[l])
        x = layer_norm(ALPHA * x + ff, ln3_g[l], ln3_b[l])
    return x
```

```python
import functools

import jax
import jax.numpy as jnp
from jax import lax
from jax.experimental import pallas as pl
from jax.experimental.pallas import tpu as pltpu

D_MODEL = 1024
N_MEM = 256
HEADS = 4
DH = 128
D_GRP = HEADS * DH
HG_CHUNK = 64
ML_CHUNK = 256
ML_CONV = 4
FFN_CONV = 3
D_FF = 2816
CA_HEADS = 4
CA_DH = D_MODEL // CA_HEADS
DEPTH = 1
ALPHA = (2.0 * DEPTH) ** 0.25
LN_EPS = 1e-5
NEG_BIG = -1e30

C_HQ, C_HF, C_HI, C_HG, C_MQ, C_MK, C_MV, C_MO = (i * D_GRP for i in range(8))
D_MAIN = 8 * D_GRP
GATE_LANES = 128

SEQ_TILE = 512
HIST = 8
VMEM_LIMIT_BYTES = 56 * 1024 * 1024

F32 = jnp.float32
BF16 = jnp.bfloat16


def _dot(a, b):
    return jnp.dot(a, b, preferred_element_type=F32)


def _dot_nt(a, b):
    return lax.dot_general(a, b, (((1,), (1,)), ((), ())), preferred_element_type=F32)


def _dot_tn(a, b):
    return lax.dot_general(a, b, (((0,), (0,)), ((), ())), preferred_element_type=F32)


def _sigmoid(v):
    return 1.0 / (1.0 + jnp.exp(-v))


def _silu(v):
    return v * _sigmoid(v)


def _log_sigmoid(v):
    return jnp.minimum(v, 0.0) - jnp.log1p(jnp.exp(-jnp.abs(v)))


def _layer_norm(z, g, b):
    mu = jnp.mean(z, axis=-1, keepdims=True)
    zc = z - mu
    var = jnp.mean(zc * zc, axis=-1, keepdims=True)
    return zc * lax.rsqrt(var + LN_EPS) * g + b


def _cumsum_rows(tril, v):
    hi = v.astype(BF16)
    r1 = v - hi.astype(F32)
    mid = r1.astype(BF16)
    lo = (r1 - mid.astype(F32)).astype(BF16)
    return _dot(tril, hi) + (_dot(tril, mid) + _dot(tril, lo))


def _tril_mask(n):
    return lax.broadcasted_iota(jnp.int32, (n, n), 0) >= lax.broadcasted_iota(jnp.int32, (n, n), 1)


def _mixer_kernel(x_ref, wm_ref, wg_ref, bm_ref, bg_ref, lbl_ref, hgw_ref, cw_ref, cb_ref,
                  mlw_ref, wo_ref, g_ref, b_ref, out_ref,
                  proj_ref, qk_ref, gate_ref, y_ref, hs_ref, mc_ref, mm_ref):
    T = SEQ_TILE

    @pl.when(pl.program_id(1) == 0)
    def _start_of_sequence():
        hs_ref[...] = jnp.zeros_like(hs_ref)
        mc_ref[...] = jnp.zeros_like(mc_ref)
        mm_ref[...] = jnp.full_like(mm_ref, NEG_BIG)
        qk_ref[0:HIST, :] = jnp.zeros((HIST, 2 * D_GRP), F32)

    xb = x_ref[...].astype(BF16)
    for grp in range(8):
        cols = slice(grp * D_GRP, (grp + 1) * D_GRP)
        p = _dot(xb, wm_ref[:, cols]) + bm_ref[:, cols]
        if grp * D_GRP == C_MQ:
            qk_ref[HIST:HIST + T, 0:D_GRP] = p
        elif grp * D_GRP == C_MK:
            qk_ref[HIST:HIST + T, D_GRP:2 * D_GRP] = p
        else:
            proj_ref[:, cols] = p
    gates = _dot(xb, wg_ref[...]) + bg_ref[...]
    lane = lax.broadcasted_iota(jnp.int32, gates.shape, 1)
    gate_ref[...] = jnp.where((lane >= HEADS) & (lane < 2 * HEADS), _log_sigmoid(gates), gates)

    conv = cb_ref[...] + qk_ref[HIST:HIST + T, :] * cw_ref[ML_CONV - 1:ML_CONV, :]
    for j in range(ML_CONV - 1):
        lo = HIST - (ML_CONV - 1) + j
        conv = conv + qk_ref[lo:lo + T, :] * cw_ref[j:j + 1, :]
    conv = _silu(conv)
    proj_ref[:, C_MQ:C_MQ + D_GRP] = conv[:, 0:D_GRP] * (DH ** -0.5)
    proj_ref[:, C_MK:C_MK + D_GRP] = conv[:, D_GRP:2 * D_GRP]
    qk_ref[0:HIST, :] = qk_ref[T:T + HIST, :]

    l0 = lbl_ref[0:1, :]
    l1 = lbl_ref[1:2, :]
    lmax = jnp.maximum(l0, l1)
    e0 = jnp.exp(l0 - lmax)
    lb = e0 / (e0 + jnp.exp(l1 - lmax))
    L = HG_CHUNK
    tril_hg = _tril_mask(L)
    tril_hg_bf = jnp.where(tril_hg, 1.0, 0.0).astype(BF16)

    def hg_chunk(c, carry):
        r = pl.multiple_of(c * L, L)
        rows = pl.ds(r, L)
        hf = proj_ref[rows, C_HF:C_HF + D_GRP]
        log_f = jnp.log(lb + (1.0 - lb) * _sigmoid(hf))
        k_in = (1.0 - lb) * _sigmoid(-hf)
        b = _cumsum_rows(tril_hg_bf, log_f)
        b_mid = b[L // 2 - 1:L // 2, :]
        b_last = b[L - 1:L, :]
        q = _silu(proj_ref[rows, C_HQ:C_HQ + D_GRP])
        qd = (q * jnp.exp(b - b_mid)).astype(BF16)
        kd = (k_in * jnp.exp(b_mid - b)).astype(BF16)
        qe = (q * jnp.exp(b)).astype(BF16)
        kl = (k_in * jnp.exp(b_last - b)).astype(BF16)
        v = proj_ref[rows, C_HI:C_HI + D_GRP].astype(BF16)
        state_decay = jnp.exp(b_last)
        for h in range(HEADS):
            cs = slice(h * DH, (h + 1) * DH)
            attn = jnp.where(tril_hg, _dot_nt(qd[:, cs], kd[:, cs]), 0.0).astype(BF16)
            st = hs_ref[h]
            o = _dot(attn, v[:, cs]) + _dot_nt(qe[:, cs], st.astype(BF16))
            hs_ref[h] = state_decay[:, cs] * st + _dot_tn(v[:, cs], kl[:, cs])
            o = o * lax.rsqrt(jnp.mean(o * o, axis=-1, keepdims=True) + LN_EPS)
            gate = _silu(proj_ref[rows, C_HG + h * DH:C_HG + (h + 1) * DH])
            y_ref[rows, cs] = (o * hgw_ref[:, cs] * gate).astype(BF16)
        return carry

    lax.fori_loop(0, T // L, hg_chunk, 0)

    M = ML_CHUNK
    tril_ml = _tril_mask(M)
    tril_ml_bf = jnp.where(tril_ml, 1.0, 0.0).astype(BF16)
    lane_m = lax.broadcasted_iota(jnp.int32, (M, GATE_LANES), 1)
    ones_col = jnp.where(lax.broadcasted_iota(jnp.int32, (M, DH), 1) == 0, 1.0, 0.0).astype(BF16)
    for c in range(T // M):
        rows = slice(c * M, (c + 1) * M)
        gc = gate_ref[rows, :]
        cum = _cumsum_rows(tril_ml_bf, gc)
        col = jnp.where(lane_m < HEADS, gc, cum)
        row = col.T
        for h in range(HEADS):
            ig_col = col[:, h:h + 1]
            b_col = col[:, HEADS + h:HEADS + h + 1]
            ig_row = row[h:h + 1, :]
            b_row = row[HEADS + h:HEADS + h + 1, :]
            m_prev = mm_ref[h:h + 1, 0:1]
            d = jnp.where(tril_ml, b_col - b_row + ig_row, -jnp.inf)
            inter = b_col + m_prev
            m_t = jnp.maximum(inter, jnp.max(d, axis=-1, keepdims=True))
            w = jnp.exp(d - m_t)
            q = proj_ref[rows, C_MQ + h * DH:C_MQ + (h + 1) * DH].astype(BF16)
            k = proj_ref[rows, C_MK + h * DH:C_MK + (h + 1) * DH]
            v = proj_ref[rows, C_MV + h * DH:C_MV + (h + 1) * DH].astype(BF16)
            v_ext = jnp.concatenate([v, ones_col], axis=1)
            s = (_dot_nt(q, k.astype(BF16)) * w).astype(BF16)
            st = mc_ref[h]
            w_inter = jnp.exp(inter - m_t)
            tot = _dot(s, v_ext) + w_inter * _dot(q, st.astype(BF16))
            num = tot[:, 0:DH]
            den = tot[:, DH:DH + 1]
            h_out = num / jnp.maximum(jnp.abs(den), jnp.exp(-m_t))
            g_tot = b_col[M - 1:M, :]
            a = g_tot - b_col + ig_col
            m_new = jnp.maximum(g_tot + m_prev, jnp.max(a, axis=0, keepdims=True))
            decay = jnp.exp(g_tot + m_prev - m_new)
            wk = (k * jnp.exp(a - m_new)).astype(BF16)
            mc_ref[h] = decay * st + _dot_tn(wk, v_ext)
            mm_ref[h:h + 1, :] = jnp.broadcast_to(m_new, (1, GATE_LANES))
            mu = jnp.mean(h_out, axis=-1, keepdims=True)
            hc = h_out - mu
            var = jnp.mean(hc * hc, axis=-1, keepdims=True)
            hn = hc * lax.rsqrt(var + LN_EPS) * mlw_ref[:, h * DH:(h + 1) * DH]
            og = _sigmoid(proj_ref[rows, C_MO + h * DH:C_MO + (h + 1) * DH])
            y_ref[rows, D_GRP + h * DH:D_GRP + (h + 1) * DH] = (og * hn).astype(BF16)

    mix = _dot(y_ref[...], wo_ref[...])
    out_ref[...] = _layer_norm(ALPHA * x_ref[...] + mix, g_ref[...], b_ref[...])


def _cross_attn_kernel(x_ref, mem_ref, wq_ref, wkv_ref, wo_ref, g_ref, b_ref, out_ref,
                       kv_ref, o_ref):
    @pl.when(pl.program_id(1) == 0)
    def _project_memory():
        kv_ref[...] = _dot(mem_ref[...].astype(BF16), wkv_ref[...]).astype(BF16)

    x = x_ref[...]
    q = (_dot(x.astype(BF16), wq_ref[...]) * (CA_DH ** -0.5)).astype(BF16)
    for h in range(CA_HEADS):
        cs = slice(h * CA_DH, (h + 1) * CA_DH)
        sc = _dot_nt(q[:, cs], kv_ref[:, cs])
        e = jnp.exp(sc - jnp.max(sc, axis=-1, keepdims=True))
        pv = _dot(e.astype(BF16), kv_ref[:, D_MODEL + h * CA_DH:D_MODEL + (h + 1) * CA_DH])
        o_ref[:, cs] = (pv / jnp.sum(e, axis=-1, keepdims=True)).astype(BF16)
    ca = _dot(o_ref[...], wo_ref[...])
    out_ref[...] = _layer_norm(ALPHA * x + ca, g_ref[...], b_ref[...])


FFN_COLS = 256


def _gelu_tanh(v):
    return 0.5 * v * (1.0 + jnp.tanh(0.7978845608028654 * (v + 0.044715 * (v * v * v))))


def _ffn_kernel(x_ref, wu_ref, cw_ref, cb_ref, wd_ref, g_ref, b_ref, out_ref, u_ref, h_ref):
    T = SEQ_TILE

    @pl.when(pl.program_id(1) == 0)
    def _start_of_sequence():
        u_ref[0:HIST, :] = jnp.zeros((HIST, 2 * D_FF), F32)

    x = x_ref[...]
    xb = x.astype(BF16)
    for blk in range(2 * D_FF // FFN_COLS):
        cols = slice(blk * FFN_COLS, (blk + 1) * FFN_COLS)
        u_ref[HIST:HIST + T, cols] = _dot(xb, wu_ref[:, cols])
    for blk in range(D_FF // FFN_COLS):
        halves = []
        for base in (0, D_FF):
            cols = slice(base + blk * FFN_COLS, base + (blk + 1) * FFN_COLS)
            acc = cb_ref[:, cols] + u_ref[HIST:HIST + T, cols] * cw_ref[FFN_CONV - 1:FFN_CONV, cols]
            for j in range(FFN_CONV - 1):
                lo = HIST - (FFN_CONV - 1) + j
                acc = acc + u_ref[lo:lo + T, cols] * cw_ref[j:j + 1, cols]
            halves.append(acc)
        gate, val = halves
        h_ref[:, blk * FFN_COLS:(blk + 1) * FFN_COLS] = (_gelu_tanh(gate) * val).astype(BF16)
    u_ref[0:HIST, :] = u_ref[T:T + HIST, :]
    ff = _dot(h_ref[...], wd_ref[...])
    out_ref[...] = _layer_norm(ALPHA * x + ff, g_ref[...], b_ref[...])


def _resident(shape):
    return pl.BlockSpec(shape, lambda b, t: (0,) * len(shape), pipeline_mode=pl.Buffered(1))


def _seq_tile_spec():
    return pl.BlockSpec((None, SEQ_TILE, D_MODEL), lambda b, t: (b, t, 0))


_COMPILER_PARAMS = pltpu.CompilerParams(
    dimension_semantics=("arbitrary", "arbitrary"), vmem_limit_bytes=VMEM_LIMIT_BYTES)


def _row(v):
    return v.reshape(1, -1).astype(F32)


def kernel(x, mem, w_in, b_in, hg_lb_logits, hg_norm_w, ml_conv_w, ml_conv_b, ml_norm_w, w_out,
           ln1_g, ln1_b, ca_wq, ca_wkv, ca_wo, ln2_g, ln2_b,
           ffn_w_up, ffn_conv_w, ffn_conv_b, ffn_w_down, ln3_g, ln3_b):
    bsz, seq, _ = x.shape
    assert seq % SEQ_TILE == 0 and SEQ_TILE % ML_CHUNK == 0 and SEQ_TILE % HG_CHUNK == 0
    grid = (bsz, seq // SEQ_TILE)
    act_shape = jax.ShapeDtypeStruct((bsz, seq, D_MODEL), F32)
    lower_logits = hg_lb_logits.astype(F32)
    for l in range(DEPTH):
        assert DEPTH == 1 and lower_logits.shape[0] == 2
        w_main = w_in[l, :, :D_MAIN].astype(BF16)
        w_gate = jnp.pad(w_in[l, :, D_MAIN:], ((0, 0), (0, GATE_LANES - 2 * HEADS))).astype(BF16)
        b_main = _row(b_in[l, :D_MAIN])
        b_gate = jnp.pad(_row(b_in[l, D_MAIN:]), ((0, 0), (0, GATE_LANES - 2 * HEADS)))
        x = pl.pallas_call(
            _mixer_kernel,
            grid=grid,
            in_specs=[
                _seq_tile_spec(),
                _resident((D_MODEL, D_MAIN)), _resident((D_MODEL, GATE_LANES)),
                _resident((1, D_MAIN)), _resident((1, GATE_LANES)),
                _resident((2, D_GRP)), _resident((1, D_GRP)),
                _resident((ML_CONV, 2 * D_GRP)), _resident((1, 2 * D_GRP)),
                _resident((1, D_GRP)), _resident((2 * D_GRP, D_MODEL)),
                _resident((1, D_MODEL)), _resident((1, D_MODEL)),
            ],
            out_specs=_seq_tile_spec(),
            out_shape=act_shape,
            scratch_shapes=[
                pltpu.VMEM((SEQ_TILE, D_MAIN), F32),
                pltpu.VMEM((HIST + SEQ_TILE, 2 * D_GRP), F32),
                pltpu.VMEM((SEQ_TILE, GATE_LANES), F32),
                pltpu.VMEM((SEQ_TILE, 2 * D_GRP), BF16),
                pltpu.VMEM((HEADS, DH, DH), F32),
                pltpu.VMEM((HEADS, DH, 2 * DH), F32),
                pltpu.VMEM((8, GATE_LANES), F32),
            ],
            compiler_params=_COMPILER_PARAMS,
            name="mixer",
        )(x, w_main, w_gate, b_main, b_gate, lower_logits, _row(hg_norm_w[l]),
          ml_conv_w[l].astype(F32), _row(ml_conv_b[l]), _row(ml_norm_w[l]),
          w_out[l].astype(BF16), _row(ln1_g[l]), _row(ln1_b[l]))

        x = pl.pallas_call(
            _cross_attn_kernel,
            grid=grid,
            in_specs=[
                _seq_tile_spec(),
                pl.BlockSpec((None, N_MEM, D_MODEL), lambda b, t: (b, 0, 0)),
                _resident((D_MODEL, D_MODEL)), _resident((D_MODEL, 2 * D_MODEL)),
                _resident((D_MODEL, D_MODEL)),
                _resident((1, D_MODEL)), _resident((1, D_MODEL)),
            ],
            out_specs=_seq_tile_spec(),
            out_shape=act_shape,
            scratch_shapes=[
                pltpu.VMEM((N_MEM, 2 * D_MODEL), BF16),
                pltpu.VMEM((SEQ_TILE, D_MODEL), BF16),
            ],
            compiler_params=_COMPILER_PARAMS,
            name="cross_attn",
        )(x, mem, ca_wq[l].astype(BF16), ca_wkv[l].astype(BF16), ca_wo[l].astype(BF16),
          _row(ln2_g[l]), _row(ln2_b[l]))

        x = pl.pallas_call(
            _ffn_kernel,
            grid=grid,
            in_specs=[
                _seq_tile_spec(),
                _resident((D_MODEL, 2 * D_FF)),
                _resident((FFN_CONV, 2 * D_FF)), _resident((1, 2 * D_FF)),
                _resident((D_FF, D_MODEL)),
                _resident((1, D_MODEL)), _resident((1, D_MODEL)),
            ],
            out_specs=_seq_tile_spec(),
            out_shape=act_shape,
            scratch_shapes=[
                pltpu.VMEM((HIST + SEQ_TILE, 2 * D_FF), F32),
                pltpu.VMEM((SEQ_TILE, D_FF), BF16),
            ],
            compiler_params=_COMPILER_PARAMS,
            name="conv_ffn",
        )(x, ffn_w_up[l].astype(BF16), ffn_conv_w[l].astype(F32), _row(ffn_conv_b[l]),
          ffn_w_down[l].astype(BF16), _row(ln3_g[l]), _row(ln3_b[l]))
    return x
```

```python
import functools

import jax
import jax.numpy as jnp
from jax import lax
from jax.experimental import pallas as pl
from jax.experimental.pallas import tpu as pltpu

D_MODEL = 1024
N_MEM = 256
HEADS = 4
DH = 128
D_GRP = HEADS * DH
HG_CHUNK = 64
ML_CHUNK = 128
ML_CONV = 4
FFN_CONV = 3
D_FF = 2816
CA_HEADS = 4
CA_DH = D_MODEL // CA_HEADS
DEPTH = 1
ALPHA = (2.0 * DEPTH) ** 0.25
LN_EPS = 1e-5
NEG_BIG = -1e30

C_HQ, C_HF, C_HI, C_HG, C_MQ, C_MK, C_MV, C_MO = (i * D_GRP for i in range(8))
D_MAIN = 8 * D_GRP
GATE_LANES = 128

SEQ_TILE = 512
HIST = 8
VMEM_LIMIT_BYTES = 56 * 1024 * 1024

F32 = jnp.float32
BF16 = jnp.bfloat16


def _dot(a, b):
    return jnp.dot(a, b, preferred_element_type=F32)


def _dot_nt(a, b):
    return lax.dot_general(a, b, (((1,), (1,)), ((), ())), preferred_element_type=F32)


def _dot_tn(a, b):
    return lax.dot_general(a, b, (((0,), (0,)), ((), ())), preferred_element_type=F32)


def _sigmoid(v):
    return 0.5 + 0.5 * jnp.tanh(0.5 * v)


def _silu(v):
    return v * _sigmoid(v)


def _log_sigmoid(v):
    return jnp.minimum(v, 0.0) - jnp.log1p(jnp.exp(-jnp.abs(v)))


def _layer_norm(z, g, b):
    mu = jnp.mean(z, axis=-1, keepdims=True)
    zc = z - mu
    var = jnp.mean(zc * zc, axis=-1, keepdims=True)
    return zc * lax.rsqrt(var + LN_EPS) * g + b


def _cumsum_rows(tril, v):
    hi = v.astype(BF16)
    r1 = v - hi.astype(F32)
    mid = r1.astype(BF16)
    lo = (r1 - mid.astype(F32)).astype(BF16)
    return _dot(tril, hi) + (_dot(tril, mid) + _dot(tril, lo))


def _tril_mask(n):
    return lax.broadcasted_iota(jnp.int32, (n, n), 0) >= lax.broadcasted_iota(jnp.int32, (n, n), 1)


def _mixer_kernel(x_ref, wm_ref, wg_ref, bm_ref, bg_ref, lbl_ref, hgw_ref, cw_ref, cb_ref,
                  mlw_ref, wo_ref, g_ref, b_ref, out_ref,
                  xb_ref, proj_ref, qk_ref, gate_ref, y_ref, hs_ref, mc_ref, mm_ref):
    T = SEQ_TILE

    @pl.when(pl.program_id(1) == 0)
    def _start_of_sequence():
        hs_ref[...] = jnp.zeros_like(hs_ref)
        mc_ref[...] = jnp.zeros_like(mc_ref)
        mm_ref[...] = jnp.full_like(mm_ref, NEG_BIG)
        qk_ref[0:HIST, :] = jnp.zeros((HIST, 2 * D_GRP), F32)

    xb_ref[...] = x_ref[...].astype(BF16)
    xb = xb_ref[...]
    for grp in range(8):
        cols = slice(grp * D_GRP, (grp + 1) * D_GRP)
        p = _dot(xb, wm_ref[:, cols]) + bm_ref[:, cols]
        if grp * D_GRP == C_MQ:
            qk_ref[HIST:HIST + T, 0:D_GRP] = p
        elif grp * D_GRP == C_MK:
            qk_ref[HIST:HIST + T, D_GRP:2 * D_GRP] = p
        else:
            proj_ref[:, cols] = p
    gates = _dot(xb, wg_ref[...]) + bg_ref[...]
    lane = lax.broadcasted_iota(jnp.int32, gates.shape, 1)
    gate_ref[...] = jnp.where((lane >= HEADS) & (lane < 2 * HEADS), _log_sigmoid(gates), gates)

    conv = cb_ref[...] + qk_ref[HIST:HIST + T, :] * cw_ref[ML_CONV - 1:ML_CONV, :]
    for j in range(ML_CONV - 1):
        lo = HIST - (ML_CONV - 1) + j
        conv = conv + qk_ref[lo:lo + T, :] * cw_ref[j:j + 1, :]
    conv = _silu(conv)
    proj_ref[:, C_MQ:C_MQ + D_GRP] = conv[:, 0:D_GRP] * (DH ** -0.5)
    proj_ref[:, C_MK:C_MK + D_GRP] = conv[:, D_GRP:2 * D_GRP]
    qk_ref[0:HIST, :] = qk_ref[T:T + HIST, :]

    l0 = lbl_ref[0:1, :]
    l1 = lbl_ref[1:2, :]
    lmax = jnp.maximum(l0, l1)
    e0 = jnp.exp(l0 - lmax)
    lb = e0 / (e0 + jnp.exp(l1 - lmax))
    one_minus_lb = 1.0 - lb
    L = HG_CHUNK
    tril_hg = _tril_mask(L)
    tril_hg_bf = jnp.where(tril_hg, 1.0, 0.0).astype(BF16)
    for c in range(T // L):
        rows = slice(c * L, (c + 1) * L)
        th = jnp.tanh(0.5 * proj_ref[rows, C_HF:C_HF + D_GRP])
        log_f = jnp.log(lb + one_minus_lb * (0.5 + 0.5 * th))
        k_in = one_minus_lb * (0.5 - 0.5 * th)
        b = _cumsum_rows(tril_hg_bf, log_f)
        b_mid = b[L // 2 - 1:L // 2, :]
        b_last = b[L - 1:L, :]
        qd = _silu(proj_ref[rows, C_HQ:C_HQ + D_GRP]) * jnp.exp(b - b_mid)
        kd = k_in * jnp.exp(b_mid - b)
        qe = (qd * jnp.exp(b_mid)).astype(BF16)
        kl = (kd * jnp.exp(b_last - b_mid)).astype(BF16)
        qd = qd.astype(BF16)
        kd = kd.astype(BF16)
        v = proj_ref[rows, C_HI:C_HI + D_GRP].astype(BF16)
        state_decay = jnp.exp(b_last)
        for h in range(HEADS):
            cs = slice(h * DH, (h + 1) * DH)
            attn = jnp.where(tril_hg, _dot_nt(qd[:, cs], kd[:, cs]), 0.0).astype(BF16)
            update = _dot_tn(v[:, cs], kl[:, cs])
            st = hs_ref[h]
            o = _dot(attn, v[:, cs]) + _dot_nt(qe[:, cs], st.astype(BF16))
            hs_ref[h] = state_decay[:, cs] * st + update
            o = o * lax.rsqrt(jnp.mean(o * o, axis=-1, keepdims=True) + LN_EPS)
            gate = _silu(proj_ref[rows, C_HG + h * DH:C_HG + (h + 1) * DH])
            y_ref[rows, cs] = (o * hgw_ref[:, cs] * gate).astype(BF16)

    M = ML_CHUNK
    tril_ml = _tril_mask(M)
    tril_ml_bf = jnp.where(tril_ml, 1.0, 0.0).astype(BF16)
    lane_m = lax.broadcasted_iota(jnp.int32, (M, GATE_LANES), 1)
    ones_col = jnp.where(lax.broadcasted_iota(jnp.int32, (M, DH), 1) == 0, 1.0, 0.0).astype(BF16)
    for c in range(T // M):
        rows = slice(c * M, (c + 1) * M)
        gc = gate_ref[rows, :]
        cum = _cumsum_rows(tril_ml_bf, gc)
        col = jnp.where(lane_m < HEADS, gc, cum)
        row = col.T
        for h in range(HEADS):
            ig_col = col[:, h:h + 1]
            b_col = col[:, HEADS + h:HEADS + h + 1]
            ig_row = row[h:h + 1, :]
            b_row = row[HEADS + h:HEADS + h + 1, :]
            q = proj_ref[rows, C_MQ + h * DH:C_MQ + (h + 1) * DH].astype(BF16)
            k = proj_ref[rows, C_MK + h * DH:C_MK + (h + 1) * DH]
            v = proj_ref[rows, C_MV + h * DH:C_MV + (h + 1) * DH].astype(BF16)
            v_ext = jnp.concatenate([v, ones_col], axis=1)
            d = jnp.where(tril_ml, b_col - b_row + ig_row, -jnp.inf)
            d_max = jnp.max(d, axis=-1, keepdims=True)
            s = (_dot_nt(q, k.astype(BF16)) * jnp.exp(d - d_max)).astype(BF16)
            intra = _dot(s, v_ext)
            g_tot = b_col[M - 1:M, :]
            a = g_tot - b_col + ig_col
            a_max = jnp.max(a, axis=0, keepdims=True)
            update = _dot_tn((k * jnp.exp(a - a_max)).astype(BF16), v_ext)
            m_prev = mm_ref[h:h + 1, 0:1]
            st = mc_ref[h]
            inter = b_col + m_prev
            m_t = jnp.maximum(inter, d_max)
            tot = jnp.exp(d_max - m_t) * intra + jnp.exp(inter - m_t) * _dot(q, st.astype(BF16))
            h_out = tot[:, 0:DH] / jnp.maximum(jnp.abs(tot[:, DH:DH + 1]), jnp.exp(-m_t))
            m_new = jnp.maximum(g_tot + m_prev, a_max)
            mc_ref[h] = jnp.exp(g_tot + m_prev - m_new) * st + jnp.exp(a_max - m_new) * update
            mm_ref[h:h + 1, :] = jnp.broadcast_to(m_new, (1, GATE_LANES))
            mu = jnp.mean(h_out, axis=-1, keepdims=True)
            hc = h_out - mu
            var = jnp.mean(hc * hc, axis=-1, keepdims=True)
            hn = hc * lax.rsqrt(var + LN_EPS) * mlw_ref[:, h * DH:(h + 1) * DH]
            og = _sigmoid(proj_ref[rows, C_MO + h * DH:C_MO + (h + 1) * DH])
            y_ref[rows, D_GRP + h * DH:D_GRP + (h + 1) * DH] = (og * hn).astype(BF16)

    mix = _dot(y_ref[...], wo_ref[...])
    out_ref[...] = _layer_norm(ALPHA * x_ref[...] + mix, g_ref[...], b_ref[...])


def _cross_attn_kernel(x_ref, mem_ref, wq_ref, wkv_ref, wo_ref, g_ref, b_ref, out_ref,
                       kv_ref, o_ref):
    @pl.when(pl.program_id(1) == 0)
    def _project_memory():
        kv_ref[...] = _dot(mem_ref[...].astype(BF16), wkv_ref[...]).astype(BF16)

    x = x_ref[...]
    q = (_dot(x.astype(BF16), wq_ref[...]) * (CA_DH ** -0.5)).astype(BF16)
    for h in range(CA_HEADS):
        cs = slice(h * CA_DH, (h + 1) * CA_DH)
        sc = _dot_nt(q[:, cs], kv_ref[:, cs])
        e = jnp.exp(sc - jnp.max(sc, axis=-1, keepdims=True))
        pv = _dot(e.astype(BF16), kv_ref[:, D_MODEL + h * CA_DH:D_MODEL + (h + 1) * CA_DH])
        o_ref[:, cs] = (pv / jnp.sum(e, axis=-1, keepdims=True)).astype(BF16)
    ca = _dot(o_ref[...], wo_ref[...])
    out_ref[...] = _layer_norm(ALPHA * x + ca, g_ref[...], b_ref[...])


FFN_COLS = 256


def _gelu_tanh(v):
    return 0.5 * v * (1.0 + jnp.tanh(0.7978845608028654 * (v + 0.044715 * (v * v * v))))


def _ffn_kernel(x_ref, wu_ref, cw_ref, cb_ref, wd_ref, g_ref, b_ref, out_ref, u_ref, h_ref):
    T = SEQ_TILE

    @pl.when(pl.program_id(1) == 0)
    def _start_of_sequence():
        u_ref[0:HIST, :] = jnp.zeros((HIST, 2 * D_FF), F32)

    x = x_ref[...]
    xb = x.astype(BF16)
    for blk in range(2 * D_FF // FFN_COLS):
        cols = slice(blk * FFN_COLS, (blk + 1) * FFN_COLS)
        u_ref[HIST:HIST + T, cols] = _dot(xb, wu_ref[:, cols])
    for blk in range(D_FF // FFN_COLS):
        halves = []
        for base in (0, D_FF):
            cols = slice(base + blk * FFN_COLS, base + (blk + 1) * FFN_COLS)
            acc = cb_ref[:, cols] + u_ref[HIST:HIST + T, cols] * cw_ref[FFN_CONV - 1:FFN_CONV, cols]
            for j in range(FFN_CONV - 1):
                lo = HIST - (FFN_CONV - 1) + j
                acc = acc + u_ref[lo:lo + T, cols] * cw_ref[j:j + 1, cols]
            halves.append(acc)
        gate, val = halves
        h_ref[:, blk * FFN_COLS:(blk + 1) * FFN_COLS] = (_gelu_tanh(gate) * val).astype(BF16)
    u_ref[0:HIST, :] = u_ref[T:T + HIST, :]
    ff = _dot(h_ref[...], wd_ref[...])
    out_ref[...] = _layer_norm(ALPHA * x + ff, g_ref[...], b_ref[...])


def _resident(shape):
    return pl.BlockSpec(shape, lambda b, t: (0,) * len(shape), pipeline_mode=pl.Buffered(1))


def _seq_tile_spec():
    return pl.BlockSpec((None, SEQ_TILE, D_MODEL), lambda b, t: (b, t, 0))


_COMPILER_PARAMS = pltpu.CompilerParams(
    dimension_semantics=("arbitrary", "arbitrary"), vmem_limit_bytes=VMEM_LIMIT_BYTES)


def _row(v):
    return v.reshape(1, -1).astype(F32)


def kernel(x, mem, w_in, b_in, hg_lb_logits, hg_norm_w, ml_conv_w, ml_conv_b, ml_norm_w, w_out,
           ln1_g, ln1_b, ca_wq, ca_wkv, ca_wo, ln2_g, ln2_b,
           ffn_w_up, ffn_conv_w, ffn_conv_b, ffn_w_down, ln3_g, ln3_b):
    bsz, seq, _ = x.shape
    assert seq % SEQ_TILE == 0 and SEQ_TILE % ML_CHUNK == 0 and SEQ_TILE % HG_CHUNK == 0
    grid = (bsz, seq // SEQ_TILE)
    act_shape = jax.ShapeDtypeStruct((bsz, seq, D_MODEL), F32)
    lower_logits = hg_lb_logits.astype(F32)
    for l in range(DEPTH):
        assert DEPTH == 1 and lower_logits.shape[0] == 2
        w_main = w_in[l, :, :D_MAIN].astype(BF16)
        w_gate = jnp.pad(w_in[l, :, D_MAIN:], ((0, 0), (0, GATE_LANES - 2 * HEADS))).astype(BF16)
        b_main = _row(b_in[l, :D_MAIN])
        b_gate = jnp.pad(_row(b_in[l, D_MAIN:]), ((0, 0), (0, GATE_LANES - 2 * HEADS)))
        x = pl.pallas_call(
            _mixer_kernel,
            grid=grid,
            in_specs=[
                _seq_tile_spec(),
                _resident((D_MODEL, D_MAIN)), _resident((D_MODEL, GATE_LANES)),
                _resident((1, D_MAIN)), _resident((1, GATE_LANES)),
                _resident((2, D_GRP)), _resident((1, D_GRP)),
                _resident((ML_CONV, 2 * D_GRP)), _resident((1, 2 * D_GRP)),
                _resident((1, D_GRP)), _resident((2 * D_GRP, D_MODEL)),
                _resident((1, D_MODEL)), _resident((1, D_MODEL)),
            ],
            out_specs=_seq_tile_spec(),
            out_shape=act_shape,
            scratch_shapes=[
                pltpu.VMEM((SEQ_TILE, D_MODEL), BF16),
                pltpu.VMEM((SEQ_TILE, D_MAIN), F32),
                pltpu.VMEM((HIST + SEQ_TILE, 2 * D_GRP), F32),
                pltpu.VMEM((SEQ_TILE, GATE_LANES), F32),
                pltpu.VMEM((SEQ_TILE, 2 * D_GRP), BF16),
                pltpu.VMEM((HEADS, DH, DH), F32),
                pltpu.VMEM((HEADS, DH, 2 * DH), F32),
                pltpu.VMEM((8, GATE_LANES), F32),
            ],
            compiler_params=_COMPILER_PARAMS,
            name="mixer",
        )(x, w_main, w_gate, b_main, b_gate, lower_logits, _row(hg_norm_w[l]),
          ml_conv_w[l].astype(F32), _row(ml_conv_b[l]), _row(ml_norm_w[l]),
          w_out[l].astype(BF16), _row(ln1_g[l]), _row(ln1_b[l]))

        x = pl.pallas_call(
            _cross_attn_kernel,
            grid=grid,
            in_specs=[
                _seq_tile_spec(),
                pl.BlockSpec((None, N_MEM, D_MODEL), lambda b, t: (b, 0, 0)),
                _resident((D_MODEL, D_MODEL)), _resident((D_MODEL, 2 * D_MODEL)),
                _resident((D_MODEL, D_MODEL)),
                _resident((1, D_MODEL)), _resident((1, D_MODEL)),
            ],
            out_specs=_seq_tile_spec(),
            out_shape=act_shape,
            scratch_shapes=[
                pltpu.VMEM((N_MEM, 2 * D_MODEL), BF16),
                pltpu.VMEM((SEQ_TILE, D_MODEL), BF16),
            ],
            compiler_params=_COMPILER_PARAMS,
            name="cross_attn",
        )(x, mem, ca_wq[l].astype(BF16), ca_wkv[l].astype(BF16), ca_wo[l].astype(BF16),
          _row(ln2_g[l]), _row(ln2_b[l]))

        x = pl.pallas_call(
            _ffn_kernel,
            grid=grid,
            in_specs=[
                _seq_tile_spec(),
                _resident((D_MODEL, 2 * D_FF)),
                _resident((FFN_CONV, 2 * D_FF)), _resident((1, 2 * D_FF)),
                _resident((D_FF, D_MODEL)),
                _resident((1, D_MODEL)), _resident((1, D_MODEL)),
            ],
            out_specs=_seq_tile_spec(),
            out_shape=act_shape,
            scratch_shapes=[
                pltpu.VMEM((HIST + SEQ_TILE, 2 * D_FF), F32),
                pltpu.VMEM((SEQ_TILE, D_FF), BF16),
            ],
            compiler_params=_COMPILER_PARAMS,
            name="conv_ffn",
        )(x, ffn_w_up[l].astype(BF16), ffn_conv_w[l].astype(F32), _row(ffn_conv_b[l]),
          ffn_w_down[l].astype(BF16), _row(ln3_g[l]), _row(ln3_b[l]))
    return x
```

```python
import jax
import jax.numpy as jnp
from jax import lax
from jax.experimental import pallas as pl
from jax.experimental.pallas import tpu as pltpu

D_MODEL = 1024
N_MEM = 256
HEADS = 4
DH = 128
D_GRP = HEADS * DH
HG_CHUNK = 64
ML_CHUNK = 128
ML_CONV = 4
FFN_CONV = 3
D_FF = 2816
CA_HEADS = 4
CA_DH = D_MODEL // CA_HEADS
DEPTH = 1
ALPHA = (2.0 * DEPTH) ** 0.25
LN_EPS = 1e-5
NEG_BIG = -1e30

C_HQ, C_HF, C_HI, C_HG, C_MQ, C_MK, C_MV, C_MO = (i * D_GRP for i in range(8))
D_MAIN = 8 * D_GRP
GATE_LANES = 128

SEQ_TILE = 512
HIST = 8
VMEM_LIMIT_BYTES = 56 * 1024 * 1024

F32 = jnp.float32
BF16 = jnp.bfloat16


def _dot(a, b):
    return jnp.dot(a, b, preferred_element_type=F32)


def _dot_nt(a, b):
    return lax.dot_general(a, b, (((1,), (1,)), ((), ())), preferred_element_type=F32)


def _dot_tn(a, b):
    return lax.dot_general(a, b, (((0,), (0,)), ((), ())), preferred_element_type=F32)


def _sigmoid(v):
    return 0.5 + 0.5 * jnp.tanh(0.5 * v)


def _silu(v):
    return v * _sigmoid(v)


def _log_sigmoid(v):
    return jnp.minimum(v, 0.0) - jnp.log1p(jnp.exp(-jnp.abs(v)))


def _layer_norm(z, g, b):
    mu = jnp.mean(z, axis=-1, keepdims=True)
    zc = z - mu
    var = jnp.mean(zc * zc, axis=-1, keepdims=True)
    return zc * lax.rsqrt(var + LN_EPS) * g + b


def _cumsum_rows(tril, v):
    hi = v.astype(BF16)
    r1 = v - hi.astype(F32)
    mid = r1.astype(BF16)
    lo = (r1 - mid.astype(F32)).astype(BF16)
    return _dot(tril, hi) + (_dot(tril, mid) + _dot(tril, lo))


def _tril_mask(n):
    return lax.broadcasted_iota(jnp.int32, (n, n), 0) >= lax.broadcasted_iota(jnp.int32, (n, n), 1)


A_HQ, A_LOGF, A_KIN, A_HG, A_MK, A_MO = (i * D_GRP for i in range(6))
D_ACT = 6 * D_GRP
B_HV, B_MQ, B_MV = (i * D_GRP for i in range(3))
D_OPER = 3 * D_GRP

HG_GROUP = 4
ML_GROUP = 2


def _hgrn2_chunks(chunks, act_ref, oper_ref, hgw_ref, y_ref, hs_ref):
    L = HG_CHUNK
    heads = range(HEADS)
    tril = _tril_mask(L)
    tril_bf = jnp.where(tril, 1.0, 0.0).astype(BF16)
    rows = {c: slice(c * L, (c + 1) * L) for c in chunks}
    cols = {h: slice(h * DH, (h + 1) * DH) for h in heads}

    b = {c: _cumsum_rows(tril_bf, act_ref[rows[c], A_LOGF:A_LOGF + D_GRP]) for c in chunks}
    yield
    qd, kd, qe, kl, decay = {}, {}, {}, {}, {}
    for c in chunks:
        b_mid = b[c][L // 2 - 1:L // 2, :]
        b_last = b[c][L - 1:L, :]
        q = act_ref[rows[c], A_HQ:A_HQ + D_GRP] * jnp.exp(b[c] - b_mid)
        k = act_ref[rows[c], A_KIN:A_KIN + D_GRP] * jnp.exp(b_mid - b[c])
        qe[c] = (q * jnp.exp(b_mid)).astype(BF16)
        kl[c] = (k * jnp.exp(b_last - b_mid)).astype(BF16)
        qd[c] = q.astype(BF16)
        kd[c] = k.astype(BF16)
        decay[c] = jnp.exp(b_last)

    def value(c, h):
        return oper_ref[rows[c], B_HV + h * DH:B_HV + (h + 1) * DH]

    yield
    pairs = [(c, h) for c in chunks for h in heads]
    scores = {(c, h): _dot_nt(qd[c][:, cols[h]], kd[c][:, cols[h]]) for c, h in pairs}
    update = {(c, h): _dot_tn(value(c, h), kl[c][:, cols[h]]) for c, h in pairs}
    yield
    intra = {(c, h): _dot(jnp.where(tril, scores[c, h], 0.0).astype(BF16), value(c, h))
             for c, h in pairs}
    yield
    state_in = {}
    for h in heads:
        st = hs_ref[h]
        for c in chunks:
            state_in[c, h] = st.astype(BF16)
            st = decay[c][:, cols[h]] * st + update[c, h]
        hs_ref[h] = st
    yield
    o = {(c, h): intra[c, h] + _dot_nt(qe[c][:, cols[h]], state_in[c, h]) for c, h in pairs}
    mean_sq = {p: jnp.mean(o[p] * o[p], axis=-1, keepdims=True) for p in pairs}
    for c, h in pairs:
        gate = act_ref[rows[c], A_HG + h * DH:A_HG + (h + 1) * DH]
        normed = o[c, h] * lax.rsqrt(mean_sq[c, h] + LN_EPS)
        y_ref[rows[c], cols[h]] = (normed * hgw_ref[:, cols[h]] * gate).astype(BF16)


def _mlstm_chunks(chunks, act_ref, oper_ref, gate_ref, mlw_ref, y_ref, mc_ref, mm_ref):
    M = ML_CHUNK
    heads = range(HEADS)
    tril = _tril_mask(M)
    tril_bf = jnp.where(tril, 1.0, 0.0).astype(BF16)
    lane = lax.broadcasted_iota(jnp.int32, (M, GATE_LANES), 1)
    ones_col = jnp.where(lax.broadcasted_iota(jnp.int32, (M, DH), 1) == 0, 1.0, 0.0).astype(BF16)
    rows = {c: slice(c * M, (c + 1) * M) for c in chunks}
    pairs = [(c, h) for c in chunks for h in heads]

    col, row = {}, {}
    for c in chunks:
        gc = gate_ref[rows[c], :]
        col[c] = jnp.where(lane < HEADS, gc, _cumsum_rows(tril_bf, gc))
        row[c] = col[c].T

    def query(c, h):
        return oper_ref[rows[c], B_MQ + h * DH:B_MQ + (h + 1) * DH]

    def key(c, h):
        return act_ref[rows[c], A_MK + h * DH:A_MK + (h + 1) * DH]

    def value_ext(c, h):
        v = oper_ref[rows[c], B_MV + h * DH:B_MV + (h + 1) * DH]
        return jnp.concatenate([v, ones_col], axis=1)

    def b_col(c, h):
        return col[c][:, HEADS + h:HEADS + h + 1]

    yield
    qk = {(c, h): _dot_nt(query(c, h), key(c, h).astype(BF16)) for c, h in pairs}
    yield
    d, d_max, a, a_max, g_tot = {}, {}, {}, {}, {}
    for c, h in pairs:
        ig_row = row[c][h:h + 1, :]
        b_row = row[c][HEADS + h:HEADS + h + 1, :]
        d[c, h] = jnp.where(tril, b_col(c, h) - b_row + ig_row, -jnp.inf)
        g_tot[c, h] = b_col(c, h)[M - 1:M, :]
        a[c, h] = g_tot[c, h] - b_col(c, h) + col[c][:, h:h + 1]
    for c, h in pairs:
        d_max[c, h] = jnp.max(d[c, h], axis=-1, keepdims=True)
        a_max[c, h] = jnp.max(a[c, h], axis=0, keepdims=True)
    s = {(c, h): (qk[c, h] * jnp.exp(d[c, h] - d_max[c, h])).astype(BF16) for c, h in pairs}
    wk = {(c, h): (key(c, h) * jnp.exp(a[c, h] - a_max[c, h])).astype(BF16) for c, h in pairs}
    yield
    intra = {(c, h): _dot(s[c, h], value_ext(c, h)) for c, h in pairs}
    update = {(c, h): _dot_tn(wk[c, h], value_ext(c, h)) for c, h in pairs}
    yield
    m_in, state_in = {}, {}
    for h in heads:
        m_prev = mm_ref[h:h + 1, 0:1]
        st = mc_ref[h]
        for c in chunks:
            m_in[c, h] = m_prev
            state_in[c, h] = st.astype(BF16)
            m_new = jnp.maximum(g_tot[c, h] + m_prev, a_max[c, h])
            st = (jnp.exp(g_tot[c, h] + m_prev - m_new) * st
                  + jnp.exp(a_max[c, h] - m_new) * update[c, h])
            m_prev = m_new
        mc_ref[h] = st
        mm_ref[h:h + 1, :] = jnp.broadcast_to(m_prev, (1, GATE_LANES))
    yield
    inter_dot = {(c, h): _dot(query(c, h), state_in[c, h]) for c, h in pairs}
    w_loc, w_int, floor = {}, {}, {}
    for c, h in pairs:
        inter = b_col(c, h) + m_in[c, h]
        m_t = jnp.maximum(inter, d_max[c, h])
        w_loc[c, h] = jnp.exp(d_max[c, h] - m_t)
        w_int[c, h] = jnp.exp(inter - m_t)
        floor[c, h] = jnp.exp(-m_t)
    inv_den = {}
    for p in pairs:
        den = w_loc[p] * intra[p][:, DH:DH + 1] + w_int[p] * inter_dot[p][:, DH:DH + 1]
        inv_den[p] = 1.0 / jnp.maximum(jnp.abs(den), floor[p])
    h_out = {p: (w_loc[p] * inv_den[p]) * intra[p][:, 0:DH]
                + (w_int[p] * inv_den[p]) * inter_dot[p][:, 0:DH] for p in pairs}
    mu = {p: jnp.mean(h_out[p], axis=-1, keepdims=True) for p in pairs}
    hc = {p: h_out[p] - mu[p] for p in pairs}
    var = {p: jnp.mean(hc[p] * hc[p], axis=-1, keepdims=True) for p in pairs}
    for c, h in pairs:
        hn = hc[c, h] * lax.rsqrt(var[c, h] + LN_EPS) * mlw_ref[:, h * DH:(h + 1) * DH]
        og = act_ref[rows[c], A_MO + h * DH:A_MO + (h + 1) * DH]
        y_ref[rows[c], D_GRP + h * DH:D_GRP + (h + 1) * DH] = (og * hn).astype(BF16)


def _mixer_kernel(x_ref, wm_ref, wg_ref, bm_ref, bg_ref, lbl_ref, hgw_ref, cw_ref, cb_ref,
                  mlw_ref, wo_ref, g_ref, b_ref, out_ref,
                  xb_ref, act_ref, oper_ref, hist_ref, gate_ref, y_ref, hs_ref, mc_ref, mm_ref):
    T = SEQ_TILE

    @pl.when(pl.program_id(1) == 0)
    def _start_of_sequence():
        hs_ref[...] = jnp.zeros_like(hs_ref)
        mc_ref[...] = jnp.zeros_like(mc_ref)
        mm_ref[...] = jnp.full_like(mm_ref, NEG_BIG)
        hist_ref[...] = jnp.zeros_like(hist_ref)

    l0 = lbl_ref[0:1, :]
    l1 = lbl_ref[1:2, :]
    lmax = jnp.maximum(l0, l1)
    e0 = jnp.exp(l0 - lmax)
    lb = e0 / (e0 + jnp.exp(l1 - lmax))
    one_minus_lb = 1.0 - lb

    xb_ref[...] = x_ref[...].astype(BF16)
    xb = xb_ref[...]

    def project(col0):
        cols = slice(col0, col0 + D_GRP)
        return _dot(xb, wm_ref[:, cols]) + bm_ref[:, cols]

    def conv_silu(p, slot):
        cs = slice(slot * D_GRP, (slot + 1) * D_GRP)
        ext = jnp.concatenate([hist_ref[slot], p], axis=0)
        hist_ref[slot] = p[T - HIST:T, :]
        conv = cb_ref[:, cs] + p * cw_ref[ML_CONV - 1:ML_CONV, cs]
        for j in range(ML_CONV - 1):
            lo = HIST - (ML_CONV - 1) + j
            conv = conv + ext[lo:lo + T, :] * cw_ref[j:j + 1, cs]
        return _silu(conv)

    def hgrn2_projections():
        act_ref[:, A_HQ:A_HQ + D_GRP] = _silu(project(C_HQ))
        yield
        th = jnp.tanh(0.5 * project(C_HF))
        act_ref[:, A_LOGF:A_LOGF + D_GRP] = jnp.log(lb + one_minus_lb * (0.5 + 0.5 * th))
        act_ref[:, A_KIN:A_KIN + D_GRP] = one_minus_lb * (0.5 - 0.5 * th)
        yield
        oper_ref[:, B_HV:B_HV + D_GRP] = project(C_HI).astype(BF16)
        yield
        act_ref[:, A_HG:A_HG + D_GRP] = _silu(project(C_HG))

    def mlstm_projections():
        oper_ref[:, B_MQ:B_MQ + D_GRP] = (conv_silu(project(C_MQ), 0) * (DH ** -0.5)).astype(BF16)
        yield
        act_ref[:, A_MK:A_MK + D_GRP] = conv_silu(project(C_MK), 1)
        yield
        oper_ref[:, B_MV:B_MV + D_GRP] = project(C_MV).astype(BF16)
        yield
        act_ref[:, A_MO:A_MO + D_GRP] = _sigmoid(project(C_MO))
        yield
        gates = _dot(xb, wg_ref[...]) + bg_ref[...]
        lane = lax.broadcasted_iota(jnp.int32, gates.shape, 1)
        gate_ref[...] = jnp.where((lane >= HEADS) & (lane < 2 * HEADS), _log_sigmoid(gates), gates)

    def output_rows(r0, n):
        rows = slice(r0, r0 + n)
        mix = _dot(y_ref[rows, :], wo_ref[...])
        yield
        out_ref[rows, :] = _layer_norm(ALPHA * x_ref[rows, :] + mix, g_ref[...], b_ref[...])

    def hgrn2(group):
        chunks = range(group * HG_GROUP, (group + 1) * HG_GROUP)
        return _hgrn2_chunks(chunks, act_ref, oper_ref, hgw_ref, y_ref, hs_ref)

    def mlstm(group):
        chunks = range(group * ML_GROUP, (group + 1) * ML_GROUP)
        return _mlstm_chunks(chunks, act_ref, oper_ref, gate_ref, mlw_ref, y_ref, mc_ref, mm_ref)

    HALF = T // 2
    _run(hgrn2_projections())
    _run(mlstm_projections(), hgrn2(0))
    _run(hgrn2(1), mlstm(0))
    _run(mlstm(1), output_rows(0, HALF))
    _run(output_rows(HALF, HALF))


def _run(*tracks):
    tracks = list(tracks)
    while tracks:
        for track in list(tracks):
            if next(track, _DONE) is _DONE:
                tracks.remove(track)


_DONE = object()


def _cross_attn_kernel(x_ref, mem_ref, wq_ref, wkv_ref, wo_ref, g_ref, b_ref, out_ref,
                       kv_ref, o_ref):
    @pl.when(pl.program_id(1) == 0)
    def _project_memory():
        kv_ref[...] = _dot(mem_ref[...].astype(BF16), wkv_ref[...]).astype(BF16)

    x = x_ref[...]
    q = (_dot(x.astype(BF16), wq_ref[...]) * (CA_DH ** -0.5)).astype(BF16)
    for h in range(CA_HEADS):
        cs = slice(h * CA_DH, (h + 1) * CA_DH)
        sc = _dot_nt(q[:, cs], kv_ref[:, cs])
        e = jnp.exp(sc - jnp.max(sc, axis=-1, keepdims=True))
        pv = _dot(e.astype(BF16), kv_ref[:, D_MODEL + h * CA_DH:D_MODEL + (h + 1) * CA_DH])
        o_ref[:, cs] = (pv / jnp.sum(e, axis=-1, keepdims=True)).astype(BF16)
    ca = _dot(o_ref[...], wo_ref[...])
    out_ref[...] = _layer_norm(ALPHA * x + ca, g_ref[...], b_ref[...])


FFN_COLS = 256


def _gelu_tanh(v):
    return 0.5 * v * (1.0 + jnp.tanh(0.7978845608028654 * (v + 0.044715 * (v * v * v))))


def _ffn_kernel(x_ref, wu_ref, cw_ref, cb_ref, wd_ref, g_ref, b_ref, out_ref, u_ref, h_ref):
    T = SEQ_TILE

    @pl.when(pl.program_id(1) == 0)
    def _start_of_sequence():
        u_ref[0:HIST, :] = jnp.zeros((HIST, 2 * D_FF), F32)

    x = x_ref[...]
    xb = x.astype(BF16)
    for blk in range(2 * D_FF // FFN_COLS):
        cols = slice(blk * FFN_COLS, (blk + 1) * FFN_COLS)
        u_ref[HIST:HIST + T, cols] = _dot(xb, wu_ref[:, cols])
    for blk in range(D_FF // FFN_COLS):
        halves = []
        for base in (0, D_FF):
            cols = slice(base + blk * FFN_COLS, base + (blk + 1) * FFN_COLS)
            acc = cb_ref[:, cols] + u_ref[HIST:HIST + T, cols] * cw_ref[FFN_CONV - 1:FFN_CONV, cols]
            for j in range(FFN_CONV - 1):
                lo = HIST - (FFN_CONV - 1) + j
                acc = acc + u_ref[lo:lo + T, cols] * cw_ref[j:j + 1, cols]
            halves.append(acc)
        gate, val = halves
        h_ref[:, blk * FFN_COLS:(blk + 1) * FFN_COLS] = (_gelu_tanh(gate) * val).astype(BF16)
    u_ref[0:HIST, :] = u_ref[T:T + HIST, :]
    ff = _dot(h_ref[...], wd_ref[...])
    out_ref[...] = _layer_norm(ALPHA * x + ff, g_ref[...], b_ref[...])


def _resident(shape):
    return pl.BlockSpec(shape, lambda b, t: (0,) * len(shape), pipeline_mode=pl.Buffered(1))


def _seq_tile_spec():
    return pl.BlockSpec((None, SEQ_TILE, D_MODEL), lambda b, t: (b, t, 0))


_COMPILER_PARAMS = pltpu.CompilerParams(
    dimension_semantics=("arbitrary", "arbitrary"), vmem_limit_bytes=VMEM_LIMIT_BYTES)


def _row(v):
    return v.reshape(1, -1).astype(F32)


def kernel(x, mem, w_in, b_in, hg_lb_logits, hg_norm_w, ml_conv_w, ml_conv_b, ml_norm_w, w_out,
           ln1_g, ln1_b, ca_wq, ca_wkv, ca_wo, ln2_g, ln2_b,
           ffn_w_up, ffn_conv_w, ffn_conv_b, ffn_w_down, ln3_g, ln3_b):
    bsz, seq, _ = x.shape
    assert seq % SEQ_TILE == 0
    assert SEQ_TILE % (ML_CHUNK * ML_GROUP) == 0 and SEQ_TILE % (HG_CHUNK * HG_GROUP) == 0
    grid = (bsz, seq // SEQ_TILE)
    act_shape = jax.ShapeDtypeStruct((bsz, seq, D_MODEL), F32)
    lower_logits = hg_lb_logits.astype(F32)
    for l in range(DEPTH):
        assert DEPTH == 1 and lower_logits.shape[0] == 2
        w_main = w_in[l, :, :D_MAIN].astype(BF16)
        w_gate = jnp.pad(w_in[l, :, D_MAIN:], ((0, 0), (0, GATE_LANES - 2 * HEADS))).astype(BF16)
        b_main = _row(b_in[l, :D_MAIN])
        b_gate = jnp.pad(_row(b_in[l, D_MAIN:]), ((0, 0), (0, GATE_LANES - 2 * HEADS)))
        x = pl.pallas_call(
            _mixer_kernel,
            grid=grid,
            in_specs=[
                _seq_tile_spec(),
                _resident((D_MODEL, D_MAIN)), _resident((D_MODEL, GATE_LANES)),
                _resident((1, D_MAIN)), _resident((1, GATE_LANES)),
                _resident((2, D_GRP)), _resident((1, D_GRP)),
                _resident((ML_CONV, 2 * D_GRP)), _resident((1, 2 * D_GRP)),
                _resident((1, D_GRP)), _resident((2 * D_GRP, D_MODEL)),
                _resident((1, D_MODEL)), _resident((1, D_MODEL)),
            ],
            out_specs=_seq_tile_spec(),
            out_shape=act_shape,
            scratch_shapes=[
                pltpu.VMEM((SEQ_TILE, D_MODEL), BF16),
                pltpu.VMEM((SEQ_TILE, D_ACT), F32),
                pltpu.VMEM((SEQ_TILE, D_OPER), BF16),
                pltpu.VMEM((2, HIST, D_GRP), F32),
                pltpu.VMEM((SEQ_TILE, GATE_LANES), F32),
                pltpu.VMEM((SEQ_TILE, 2 * D_GRP), BF16),
                pltpu.VMEM((HEADS, DH, DH), F32),
                pltpu.VMEM((HEADS, DH, 2 * DH), F32),
                pltpu.VMEM((8, GATE_LANES), F32),
            ],
            compiler_params=_COMPILER_PARAMS,
            name="mixer",
        )(x, w_main, w_gate, b_main, b_gate, lower_logits, _row(hg_norm_w[l]),
          ml_conv_w[l].astype(F32), _row(ml_conv_b[l]), _row(ml_norm_w[l]),
          w_out[l].astype(BF16), _row(ln1_g[l]), _row(ln1_b[l]))

        x = pl.pallas_call(
            _cross_attn_kernel,
            grid=grid,
            in_specs=[
                _seq_tile_spec(),
                pl.BlockSpec((None, N_MEM, D_MODEL), lambda b, t: (b, 0, 0)),
                _resident((D_MODEL, D_MODEL)), _resident((D_MODEL, 2 * D_MODEL)),
                _resident((D_MODEL, D_MODEL)),
                _resident((1, D_MODEL)), _resident((1, D_MODEL)),
            ],
            out_specs=_seq_tile_spec(),
            out_shape=act_shape,
            scratch_shapes=[
                pltpu.VMEM((N_MEM, 2 * D_MODEL), BF16),
                pltpu.VMEM((SEQ_TILE, D_MODEL), BF16),
            ],
            compiler_params=_COMPILER_PARAMS,
            name="cross_attn",
        )(x, mem, ca_wq[l].astype(BF16), ca_wkv[l].astype(BF16), ca_wo[l].astype(BF16),
          _row(ln2_g[l]), _row(ln2_b[l]))

        x = pl.pallas_call(
            _ffn_kernel,
            grid=grid,
            in_specs=[
                _seq_tile_spec(),
                _resident((D_MODEL, 2 * D_FF)),
                _resident((FFN_CONV, 2 * D_FF)), _resident((1, 2 * D_FF)),
                _resident((D_FF, D_MODEL)),
                _resident((1, D_MODEL)), _resident((1, D_MODEL)),
            ],
            out_specs=_seq_tile_spec(),
            out_shape=act_shape,
            scratch_shapes=[
                pltpu.VMEM((HIST + SEQ_TILE, 2 * D_FF), F32),
                pltpu.VMEM((SEQ_TILE, D_FF), BF16),
            ],
            compiler_params=_COMPILER_PARAMS,
            name="conv_ffn",
        )(x, ffn_w_up[l].astype(BF16), ffn_conv_w[l].astype(F32), _row(ffn_conv_b[l]),
          ffn_w_down[l].astype(BF16), _row(ln3_g[l]), _row(ln3_b[l]))
    return x
```
